```python
import math
import jax, jax.numpy as jnp
from jax import lax
import numpy as np

D_MODEL = 4096
BATCH = 2
SEQ = 8192
DEPTH = 4

N_MIXERS = 2
N_A_LAYERS = (DEPTH + 1) // 2
N_B_LAYERS = DEPTH // 2

N_META = 16
BLOCK = 128
LEAD = 128
WINDOW = 128
EPS = 1e-6

A_HEAD_DIM = 128
A_Q_HEADS = D_MODEL // A_HEAD_DIM
A_KV_HEADS = A_Q_HEADS // 4
A_GROUP = A_Q_HEADS // A_KV_HEADS
A_IN_COLS = (A_Q_HEADS + 2 * A_KV_HEADS) * A_HEAD_DIM

REL_BUCKETS = 32
REL_MAX_DIST = 128

B_HEADS = 8
B_KEY_DIM = D_MODEL // 2 // B_HEADS
B_VAL_DIM = D_MODEL // B_HEADS
B_GATE_RANK = 16
B_GATE_TEMP = 16.0
B_CHUNK = 64
B_IN_COLS = 2 * B_HEADS * B_KEY_DIM + 2 * B_HEADS * B_VAL_DIM + B_GATE_RANK

D_FF = 4 * D_MODEL

kernel_name = "hybrid_swa_sink_gla_relu2_meta"


def rms_norm(x, g):
    xf = x.astype(jnp.float32)
    y = xf * lax.rsqrt(jnp.mean(xf * xf, axis=-1, keepdims=True) + EPS)
    return (y * g.astype(jnp.float32)).astype(x.dtype)


def t5_bucket(dist):
    dist = jnp.maximum(dist, 0)
    max_exact = REL_BUCKETS // 2
    log_ratio = jnp.log(jnp.maximum(dist, 1).astype(jnp.float32) / max_exact) / math.log(REL_MAX_DIST / max_exact)
    large = max_exact + (log_ratio * (REL_BUCKETS - max_exact)).astype(jnp.int32)
    large = jnp.minimum(large, REL_BUCKETS - 1)
    return jnp.where(dist < max_exact, dist, large)


def swa_mixer(h, w_in, w_out, q_gain, k_gain, sinks, rel_bias):
    bsz, L, _ = h.shape
    nb = L // BLOCK
    proj = jnp.einsum('bld,dc->blc', h, w_in)
    q, k, v = jnp.split(proj, [A_Q_HEADS * A_HEAD_DIM, (A_Q_HEADS + A_KV_HEADS) * A_HEAD_DIM], axis=-1)
    q = rms_norm(q.reshape(bsz, nb, BLOCK, A_KV_HEADS, A_GROUP, A_HEAD_DIM), q_gain) * (A_HEAD_DIM ** -0.5)
    k = rms_norm(k.reshape(bsz, L, A_KV_HEADS, A_HEAD_DIM), k_gain)
    v = v.reshape(bsz, L, A_KV_HEADS, A_HEAD_DIM)

    def band(t):
        tb = t.reshape(bsz, nb, BLOCK, A_KV_HEADS, A_HEAD_DIM)
        prev = jnp.pad(tb[:, :-1], ((0, 0), (1, 0), (0, 0), (0, 0), (0, 0)))
        return jnp.concatenate([prev, tb], axis=2)

    k_band, v_band = band(k), band(v)
    k_meta = k[:, LEAD - N_META:LEAD]
    v_meta = v[:, LEAD - N_META:LEAD]

    n_idx = jnp.arange(nb)[:, None, None]
    i_idx = jnp.arange(BLOCK)[None, :, None]
    j_idx = jnp.arange(2 * BLOCK)[None, None, :]
    m_idx = jnp.arange(N_META)[None, None, :]
    dist_band = i_idx + BLOCK - j_idx
    key_abs = n_idx * BLOCK + j_idx - BLOCK
    mask_band = (dist_band >= 0) & (dist_band < WINDOW) & (key_abs >= LEAD)
    dist_meta = n_idx * BLOCK + i_idx - (LEAD - N_META + m_idx)
    mask_meta = dist_meta >= 0

    bias_band = rel_bias[t5_bucket(dist_band[0])].astype(jnp.float32)
    bias_band = jnp.transpose(bias_band, (2, 0, 1)).reshape(A_KV_HEADS, A_GROUP, BLOCK, 2 * BLOCK)
    bias_meta = rel_bias[t5_bucket(dist_meta)].astype(jnp.float32)
    bias_meta = jnp.transpose(bias_meta, (0, 3, 1, 2)).reshape(nb, A_KV_HEADS, A_GROUP, BLOCK, N_META)

    s_band = jnp.einsum('bnqhgd,bnkhd->bnhgqk', q, k_band).astype(jnp.float32) + bias_band
    s_band = jnp.where(mask_band[None, :, None, None], s_band, -jnp.inf)
    s_meta = jnp.einsum('bnqhgd,bmhd->bnhgqm', q, k_meta).astype(jnp.float32) + bias_meta[None]
    s_meta = jnp.where(mask_meta[None, :, None, None], s_meta, -jnp.inf)
    sink = jnp.broadcast_to(sinks.astype(jnp.float32).reshape(1, 1, A_KV_HEADS, A_GROUP, 1, 1),
                            s_band.shape[:-1] + (1,))
    p = jax.nn.softmax(jnp.concatenate([s_meta, s_band, sink], axis=-1), axis=-1)
    p_meta = p[..., :N_META].astype(v.dtype)
    p_band = p[..., N_META:N_META + 2 * BLOCK].astype(v.dtype)
    o = (jnp.einsum('bnhgqm,bmhd->bnqhgd', p_meta, v_meta)
         + jnp.einsum('bnhgqk,bnkhd->bnqhgd', p_band, v_band))
    o = o.reshape(bsz, L, A_Q_HEADS * A_HEAD_DIM)
    return jnp.einsum('blc,cd->bld', o, w_out)


def gla_mixer(h, valid, w_in, w_gate2, gate_bias, o_gain, w_out):
    bsz, L, _ = h.shape
    nc = L // B_CHUNK
    hk, hv = B_HEADS * B_KEY_DIM, B_HEADS * B_VAL_DIM
    proj = jnp.einsum('bld,dc->blc', h, w_in)
    q, k, v, g, gr = jnp.split(proj, [hk, 2 * hk, 2 * hk + hv, 2 * hk + 2 * hv], axis=-1)
    shp_k = (bsz, nc, B_CHUNK, B_HEADS, B_KEY_DIM)
    shp_v = (bsz, nc, B_CHUNK, B_HEADS, B_VAL_DIM)
    qc = q.astype(jnp.float32).reshape(shp_k) * (B_KEY_DIM ** -0.5)
    kc = (k * valid[None, :, None]).astype(jnp.float32).reshape(shp_k)
    vc = v.astype(jnp.float32).reshape(shp_v)
    glog = jax.nn.log_sigmoid((jnp.einsum('blr,rc->blc', gr, w_gate2) + gate_bias).astype(jnp.float32)) / B_GATE_TEMP
    bcum = jnp.cumsum(glog.reshape(shp_k), axis=2)
    b_last = bcum[:, :, -1]
    q_dec = qc * jnp.exp(bcum)
    k_inv = kc * jnp.exp(-bcum)
    k_tail = kc * jnp.exp(b_last[:, :, None] - bcum)

    causal = jnp.tril(jnp.ones((B_CHUNK, B_CHUNK), dtype=bool))
    att = jnp.einsum('bnihd,bnjhd->bnhij', q_dec, k_inv)
    att = jnp.where(causal, att, 0.0)
    o_intra = jnp.einsum('bnhij,bnjhe->bnihe', att, vc)

    def step(state, inp):
        q_n, k_n, v_n, bl = inp
        o_n = jnp.einsum('bihd,bhde->bihe', q_n, state)
        state = state * jnp.exp(bl)[..., None] + jnp.einsum('bjhd,bjhe->bhde', k_n, v_n)
        return state, o_n

    s0 = jnp.zeros((bsz, B_HEADS, B_KEY_DIM, B_VAL_DIM), jnp.float32)
    xs = (jnp.moveaxis(q_dec, 1, 0), jnp.moveaxis(k_tail, 1, 0), jnp.moveaxis(vc, 1, 0), jnp.moveaxis(b_last, 1, 0))
    _, o_inter = lax.scan(step, s0, xs)
    o = (o_intra + jnp.moveaxis(o_inter, 0, 1)).reshape(bsz, L, B_HEADS, B_VAL_DIM)
    o = rms_norm(o, o_gain) * jax.nn.silu(g.astype(jnp.float32).reshape(bsz, L, B_HEADS, B_VAL_DIM))
    o = o.reshape(bsz, L, hv).astype(h.dtype)
    return jnp.einsum('blc,cd->bld', o, w_out)


def relu2_mlp(h, w1, w2):
    u = jnp.einsum('bld,df->blf', h, w1)
    return jnp.einsum('blf,fd->bld', jnp.square(jax.nn.relu(u)), w2)


def setup_inputs(seed: int = 0) -> dict:
    key = jax.random.key(seed)
    ks = jax.random.split(key, 17)
    f32 = jnp.float32
    nrm = lambda k, shp, s: jax.random.normal(k, shp, f32) * s
    return {
        "x": nrm(ks[0], (BATCH, SEQ, D_MODEL), 1.0),
        "meta_tokens": nrm(ks[1], (N_META, D_MODEL), 1.0),
        "rel_bias": nrm(ks[2], (REL_BUCKETS, A_Q_HEADS), 0.5),
        "mixer_norm": 1.0 + nrm(ks[3], (DEPTH, D_MODEL), 0.02),
        "mlp_norm": 1.0 + nrm(ks[4], (DEPTH, D_MODEL), 0.02),
        "a_w_in": nrm(ks[5], (N_A_LAYERS, D_MODEL, A_IN_COLS), D_MODEL ** -0.5),
        "a_q_gain": 1.0 + nrm(ks[6], (N_A_LAYERS, A_HEAD_DIM), 0.02),
        "a_k_gain": 1.0 + nrm(ks[7], (N_A_LAYERS, A_HEAD_DIM), 0.02),
        "a_sinks": nrm(ks[8], (N_A_LAYERS, A_Q_HEADS), 1.0),
        "a_w_out": nrm(ks[9], (N_A_LAYERS, A_Q_HEADS * A_HEAD_DIM, D_MODEL), (A_Q_HEADS * A_HEAD_DIM) ** -0.5),
        "b_w_in": nrm(ks[10], (N_B_LAYERS, D_MODEL, B_IN_COLS), D_MODEL ** -0.5),
        "b_w_gate2": nrm(ks[11], (N_B_LAYERS, B_GATE_RANK, B_HEADS * B_KEY_DIM), B_GATE_RANK ** -0.5),
        "b_gate_bias": nrm(ks[12], (N_B_LAYERS, B_HEADS * B_KEY_DIM), 0.1),
        "b_o_gain": 1.0 + nrm(ks[13], (N_B_LAYERS, B_VAL_DIM), 0.02),
        "b_w_out": nrm(ks[14], (N_B_LAYERS, B_HEADS * B_VAL_DIM, D_MODEL), (B_HEADS * B_VAL_DIM) ** -0.5),
        "mlp_w1": nrm(ks[15], (DEPTH, D_MODEL, D_FF), D_MODEL ** -0.5),
        "mlp_w2": nrm(ks[16], (DEPTH, D_FF, D_MODEL), D_FF ** -0.5),
    }


def reference(x, meta_tokens, rel_bias, mixer_norm, mlp_norm, a_w_in, a_q_gain, a_k_gain, a_sinks, a_w_out,
              b_w_in, b_w_gate2, b_gate_bias, b_o_gain, b_w_out, mlp_w1, mlp_w2):
    bsz = x.shape[0]
    pad = jnp.zeros((bsz, LEAD - N_META, D_MODEL), x.dtype)
    meta = jnp.broadcast_to(meta_tokens.astype(x.dtype)[None], (bsz, N_META, D_MODEL))
    h = jnp.concatenate([pad, meta, x], axis=1)
    L = h.shape[1]
    valid = (jnp.arange(L) >= LEAD - N_META).astype(x.dtype)
    for layer in range(DEPTH):
        hn = rms_norm(h, mixer_norm[layer])
        idx = layer // N_MIXERS
        if layer % N_MIXERS == 0:
            mix = swa_mixer(hn, a_w_in[idx], a_w_out[idx], a_q_gain[idx], a_k_gain[idx], a_sinks[idx], rel_bias)
        else:
            mix = gla_mixer(hn, valid, b_w_in[idx], b_w_gate2[idx], b_gate_bias[idx], b_o_gain[idx], b_w_out[idx])
        h = h + mix.astype(h.dtype)
        h = h + relu2_mlp(rms_norm(h, mlp_norm[layer]), mlp_w1[layer], mlp_w2[layer]).astype(h.dtype)
    return h[:, LEAD:]
```

```python
import functools
import math

import jax
import jax.numpy as jnp
from jax import lax
from jax.experimental import pallas as pl
from jax.experimental.pallas import tpu as pltpu

N_META = 16
BLOCK = 128
LEAD = 128
WINDOW = 128
EPS = 1e-6
A_HEAD_DIM = 128
A_GROUP = 4
REL_BUCKETS = 32
REL_MAX_DIST = 128
B_HEADS = 8
B_GATE_RANK = 16
B_GATE_TEMP = 16.0
B_CHUNK = 64
N_MIXERS = 2

LANES = 128
V7X_VMEM_BYTES = 64 * 2**20
VMEM_LIMIT = V7X_VMEM_BYTES - 8 * 2**20

F32 = jnp.float32
BF16 = jnp.bfloat16


def _pick_tile(n, candidates):
    for c in candidates:
        if n % c == 0:
            return c
    raise ValueError(f"no tile in {candidates} divides {n}")


def _params(semantics):
    return pltpu.CompilerParams(dimension_semantics=semantics, vmem_limit_bytes=VMEM_LIMIT)


def _rms_rows(x, g):
    y = x * lax.rsqrt(jnp.mean(x * x, axis=-1, keepdims=True) + EPS)
    return y * g


NORM_ROWS = 32


def _norm_rows_to(x_ref, g_ref, hn_ref):
    g = g_ref[...]

    def body(c, carry):
        r = pl.ds(pl.multiple_of(c * NORM_ROWS, NORM_ROWS), NORM_ROWS)
        hn_ref[r, :] = _rms_rows(x_ref[r, :], g).astype(hn_ref.dtype)
        return carry

    lax.fori_loop(0, x_ref.shape[0] // NORM_ROWS, body, 0)


def _norm_matmul_kernel(x_ref, g_ref, w_ref, o_ref, hn_ref):
    @pl.when(pl.program_id(1) == 0)
    def _():
        _norm_rows_to(x_ref, g_ref, hn_ref)

    o_ref[...] = jnp.dot(hn_ref[...], w_ref[...], preferred_element_type=F32).astype(o_ref.dtype)


def norm_matmul(h, gain, w, out_dtype):
    R, D = h.shape
    N = w.shape[1]
    tm = _pick_tile(R, (640, 416, 320, 128))
    tn = _pick_tile(N, (1024, 512, 256, 128))
    return pl.pallas_call(
        _norm_matmul_kernel,
        grid=(R // tm, N // tn),
        in_specs=[
            pl.BlockSpec((tm, D), lambda i, j: (i, 0)),
            pl.BlockSpec((1, D), lambda i, j: (0, 0)),
            pl.BlockSpec((D, tn), lambda i, j: (0, j)),
        ],
        out_specs=pl.BlockSpec((tm, tn), lambda i, j: (i, j)),
        out_shape=jax.ShapeDtypeStruct((R, N), out_dtype),
        scratch_shapes=[pltpu.VMEM((tm, D), BF16)],
        compiler_params=_params(("parallel", "arbitrary")),
        name="norm_matmul",
    )(h, gain.reshape(1, D), w)


def _matmul_residual_kernel(a_ref, w_ref, h_ref, o_ref):
    o_ref[...] = h_ref[...] + jnp.dot(a_ref[...], w_ref[...], preferred_element_type=F32)


def matmul_residual(a, w, h):
    R, K = a.shape
    N = w.shape[1]
    tm = _pick_tile(R, (640, 416, 320, 128))
    tn = _pick_tile(N, (1024, 512, 256, 128))
    return pl.pallas_call(
        _matmul_residual_kernel,
        grid=(R // tm, N // tn),
        in_specs=[
            pl.BlockSpec((tm, K), lambda i, j: (i, 0)),
            pl.BlockSpec((K, tn), lambda i, j: (0, j)),
            pl.BlockSpec((tm, tn), lambda i, j: (i, j)),
        ],
        out_specs=pl.BlockSpec((tm, tn), lambda i, j: (i, j)),
        out_shape=jax.ShapeDtypeStruct((R, N), F32),
        compiler_params=_params(("parallel", "parallel")),
        name="matmul_residual",
    )(a, w, h)


def _mlp_kernel(h_ref, g_ref, w1_ref, w2_ref, o_ref, hn_ref):
    @pl.when(pl.program_id(1) == 0)
    def _():
        _norm_rows_to(h_ref, g_ref, hn_ref)
        o_ref[...] = h_ref[...]

    u = jnp.dot(hn_ref[...], w1_ref[...], preferred_element_type=F32)
    a = jnp.square(jnp.maximum(u, 0.0)).astype(BF16)
    o_ref[...] += jnp.dot(a, w2_ref[...], preferred_element_type=F32)


def mlp(h, gain, w1, w2):
    R, D = h.shape
    F = w1.shape[1]
    tm = _pick_tile(R, (416, 320, 128))
    tf = _pick_tile(F, (512, 256, 128))
    return pl.pallas_call(
        _mlp_kernel,
        grid=(R // tm, F // tf),
        in_specs=[
            pl.BlockSpec((tm, D), lambda i, j: (i, 0)),
            pl.BlockSpec((1, D), lambda i, j: (0, 0)),
            pl.BlockSpec((D, tf), lambda i, j: (0, j)),
            pl.BlockSpec((tf, D), lambda i, j: (j, 0)),
        ],
        out_specs=pl.BlockSpec((tm, D), lambda i, j: (i, 0)),
        out_shape=jax.ShapeDtypeStruct((R, D), F32),
        scratch_shapes=[pltpu.VMEM((tm, D), BF16)],
        compiler_params=_params(("parallel", "arbitrary")),
        name="mlp",
    )(h, gain.reshape(1, D), w1, w2)


def _t5_bucket(dist):
    dist = jnp.maximum(dist, 0)
    max_exact = REL_BUCKETS // 2
    log_ratio = jnp.log(jnp.maximum(dist, 1).astype(F32) / max_exact) / math.log(REL_MAX_DIST / max_exact)
    large = max_exact + (log_ratio * (REL_BUCKETS - max_exact)).astype(jnp.int32)
    large = jnp.minimum(large, REL_BUCKETS - 1)
    return jnp.where(dist < max_exact, dist, large)


def _attention_bias_tables(rel_bias, nb):
    i_idx = jnp.arange(BLOCK)[:, None]
    j_idx = jnp.arange(2 * BLOCK)[None, :]
    band = rel_bias[_t5_bucket(i_idx + BLOCK - j_idx)].astype(F32)
    band = jnp.transpose(band, (2, 0, 1))
    n_idx = jnp.arange(nb)[:, None, None]
    m_idx = jnp.arange(N_META)[None, None, :]
    dist_meta = n_idx * BLOCK + i_idx[None] - (LEAD - N_META + m_idx)
    meta = rel_bias[_t5_bucket(dist_meta)].astype(F32)
    meta = jnp.transpose(meta, (0, 3, 1, 2))
    meta = jnp.pad(meta, ((0, 0), (0, 0), (0, 0), (0, LANES - N_META)))
    return band, meta


def _swa_kernel(sink_ref, q_ref, kp_ref, kc_ref, vp_ref, vc_ref, km_ref, vm_ref,
                bb_ref, bm_ref, qg_ref, kg_ref, o_ref):
    n = pl.program_id(1)
    hd = pl.program_id(2)
    dh = A_HEAD_DIM
    rows = A_GROUP * BLOCK

    qg = qg_ref[...]
    kg = kg_ref[...]
    q = q_ref[...]
    qn = jnp.concatenate(
        [_rms_rows(q[:, g * dh:(g + 1) * dh], qg) * (dh ** -0.5) for g in range(A_GROUP)], axis=0
    ).astype(BF16)
    pad_rows = jnp.zeros((LANES - N_META, dh), F32)
    keys = jnp.concatenate([km_ref[...], pad_rows, kp_ref[...], kc_ref[...]], axis=0)
    kn = _rms_rows(keys, kg).astype(BF16)
    vals = jnp.concatenate([vm_ref[...], pad_rows, vp_ref[...], vc_ref[...]], axis=0).astype(BF16)

    s = lax.dot_general(qn, kn, (((1,), (1,)), ((), ())), preferred_element_type=F32)
    bias_meta = bm_ref[0].reshape(rows, LANES)
    bias_band = bb_ref[pl.ds(hd * A_GROUP, A_GROUP)].reshape(rows, 2 * BLOCK)
    i_meta = lax.broadcasted_iota(jnp.int32, (rows, LANES), 0) % BLOCK
    m_idx = lax.broadcasted_iota(jnp.int32, (rows, LANES), 1)
    dist_meta = n * BLOCK + i_meta - (LEAD - N_META + m_idx)
    mask_meta = (m_idx < N_META) & (dist_meta >= 0)
    s_meta = jnp.where(mask_meta, s[:, :LANES] + bias_meta, -jnp.inf)
    i_idx = lax.broadcasted_iota(jnp.int32, (rows, 2 * BLOCK), 0) % BLOCK
    j_idx = lax.broadcasted_iota(jnp.int32, (rows, 2 * BLOCK), 1)
    dist_band = i_idx + BLOCK - j_idx
    key_abs = n * BLOCK + j_idx - BLOCK
    mask_band = (dist_band >= 0) & (dist_band < WINDOW) & (key_abs >= LEAD)
    s_band = jnp.where(mask_band, s[:, LANES:] + bias_band, -jnp.inf)
    s = jnp.concatenate([s_meta, s_band], axis=1)

    sink = jnp.concatenate(
        [jnp.full((BLOCK, 1), sink_ref[hd * A_GROUP + g], F32) for g in range(A_GROUP)], axis=0)
    mx = jnp.maximum(jnp.max(s, axis=-1, keepdims=True), sink)
    e = jnp.exp(s - mx)
    denom = jnp.sum(e, axis=-1, keepdims=True) + jnp.exp(sink - mx)
    p = (e / denom).astype(BF16)
    o = jnp.dot(p, vals, preferred_element_type=F32)
    for g in range(A_GROUP):
        o_ref[:, g * dh:(g + 1) * dh] = o[g * BLOCK:(g + 1) * BLOCK].astype(o_ref.dtype)


def swa_attention(proj, bsz, L, q_gain, k_gain, sinks, bias_band, bias_meta):
    dh = A_HEAD_DIM
    R = proj.shape[0]
    n_q = bias_band.shape[0]
    n_kv = n_q // A_GROUP
    nb = L // BLOCK
    gw = A_GROUP * dh
    k_col = n_q
    v_col = n_q + n_kv
    meta_blk = (LEAD - N_META) // N_META
    rows_per_batch_meta = L // N_META

    def row(b, n, h):
        return b * nb + n

    def prev(b, n, h):
        return b * nb + jnp.maximum(n - 1, 0)

    in_specs = [
        pl.BlockSpec(memory_space=pltpu.SMEM),
        pl.BlockSpec((BLOCK, gw), lambda b, n, h: (row(b, n, h), h)),
        pl.BlockSpec((BLOCK, dh), lambda b, n, h: (prev(b, n, h), k_col + h)),
        pl.BlockSpec((BLOCK, dh), lambda b, n, h: (row(b, n, h), k_col + h)),
        pl.BlockSpec((BLOCK, dh), lambda b, n, h: (prev(b, n, h), v_col + h)),
        pl.BlockSpec((BLOCK, dh), lambda b, n, h: (row(b, n, h), v_col + h)),
        pl.BlockSpec((N_META, dh), lambda b, n, h: (b * rows_per_batch_meta + meta_blk, k_col + h)),
        pl.BlockSpec((N_META, dh), lambda b, n, h: (b * rows_per_batch_meta + meta_blk, v_col + h)),
        pl.BlockSpec((n_q, BLOCK, 2 * BLOCK), lambda b, n, h: (0, 0, 0)),
        pl.BlockSpec((1, A_GROUP, BLOCK, LANES), lambda b, n, h: (n, h, 0, 0)),
        pl.BlockSpec((1, dh), lambda b, n, h: (0, 0)),
        pl.BlockSpec((1, dh), lambda b, n, h: (0, 0)),
    ]
    return pl.pallas_call(
        _swa_kernel,
        grid=(bsz, nb, n_kv),
        in_specs=in_specs,
        out_specs=pl.BlockSpec((BLOCK, gw), lambda b, n, h: (row(b, n, h), h)),
        out_shape=jax.ShapeDtypeStruct((R, n_q * dh), BF16),
        compiler_params=_params(("parallel", "parallel", "parallel")),
        name="swa_attention",
    )(sinks, proj, proj, proj, proj, proj, proj, proj, bias_band, bias_meta,
      q_gain.reshape(1, dh), k_gain.reshape(1, dh))


def _gla_kernel(q_ref, k_ref, v_ref, g_ref, gr_ref, w2_ref, gb_ref, og_ref, o_ref, state_ref):
    t = pl.program_id(2)
    T, dk = q_ref.shape
    dv = v_ref.shape[1]
    nc = T // B_CHUNK

    @pl.when(t == 0)
    def _():
        state_ref[...] = jnp.zeros_like(state_ref)

    gpre = jnp.dot(gr_ref[...].astype(BF16), w2_ref[0], preferred_element_type=F32) + gb_ref[0]
    glog = (jnp.minimum(gpre, 0.0) - jnp.log1p(jnp.exp(-jnp.abs(gpre)))) / B_GATE_TEMP
    row = lax.broadcasted_iota(jnp.int32, (T, dk), 0)
    in_chunk = row % B_CHUNK
    bcum = glog
    shift = 1
    while shift < B_CHUNK:
        bcum = bcum + jnp.where(in_chunk >= shift, pltpu.roll(bcum, shift, 0), 0.0)
        shift *= 2
    bcum3 = bcum.reshape(nc, B_CHUNK, dk)
    b_last = bcum3[:, B_CHUNK - 1:B_CHUNK, :]

    valid = (t * T + row >= LEAD - N_META).astype(F32)
    qc = q_ref[...] * (dk ** -0.5)
    kc = k_ref[...] * valid
    q_dec = (qc * jnp.exp(bcum)).astype(BF16).reshape(nc, B_CHUNK, dk)
    k_inv = (kc * jnp.exp(-bcum)).astype(BF16).reshape(nc, B_CHUNK, dk)
    k_tail = (kc.reshape(nc, B_CHUNK, dk) * jnp.exp(b_last - bcum3)).astype(BF16)
    v3 = v_ref[...].astype(BF16).reshape(nc, B_CHUNK, dv)

    att = jnp.einsum('cid,cjd->cij', q_dec, k_inv, preferred_element_type=F32)
    ci = lax.broadcasted_iota(jnp.int32, (nc, B_CHUNK, B_CHUNK), 1)
    cj = lax.broadcasted_iota(jnp.int32, (nc, B_CHUNK, B_CHUNK), 2)
    att = jnp.where(ci >= cj, att, 0.0).astype(BF16)
    o_intra = jnp.einsum('cij,cje->cie', att, v3, preferred_element_type=F32)

    eye = lax.broadcasted_iota(jnp.int32, (dk, dk), 0) == lax.broadcasted_iota(jnp.int32, (dk, dk), 1)
    decay_rows = jnp.exp(b_last)
    state = state_ref[...]
    o_inter = []
    for c in range(nc):
        o_inter.append(jnp.dot(q_dec[c], state.astype(BF16), preferred_element_type=F32))
        decay_col = jnp.sum(jnp.where(eye, jnp.broadcast_to(decay_rows[c], (dk, dk)), 0.0), axis=1, keepdims=True)
        kv = lax.dot_general(k_tail[c], v3[c], (((0,), (0,)), ((), ())), preferred_element_type=F32)
        state = state * decay_col + kv
    state_ref[...] = state
    o = o_intra.reshape(T, dv) + jnp.concatenate(o_inter, axis=0)

    g = g_ref[...]
    o = _rms_rows(o, og_ref[...]) * (g * (1.0 / (1.0 + jnp.exp(-g))))
    o_ref[...] = o.astype(o_ref.dtype)


def gla(proj, gr, w_gate2, gate_bias, o_gain, bsz, L):
    R = proj.shape[0]
    H = B_HEADS
    dk = w_gate2.shape[-1]
    dv = o_gain.shape[-1]
    T = _pick_tile(L, (640, 320, 128, 64))
    nt = L // T
    kq = (H * dk) // dk
    kv_ = (2 * H * dk) // dv
    kg = (2 * H * dk + H * dv) // dv
    return pl.pallas_call(
        _gla_kernel,
        grid=(bsz, H, nt),
        in_specs=[
            pl.BlockSpec((T, dk), lambda b, h, t: (b * nt + t, h)),
            pl.BlockSpec((T, dk), lambda b, h, t: (b * nt + t, kq + h)),
            pl.BlockSpec((T, dv), lambda b, h, t: (b * nt + t, kv_ + h)),
            pl.BlockSpec((T, dv), lambda b, h, t: (b * nt + t, kg + h)),
            pl.BlockSpec((T, LANES), lambda b, h, t: (b * nt + t, 0)),
            pl.BlockSpec((1, LANES, dk), lambda b, h, t: (h, 0, 0)),
            pl.BlockSpec((1, 1, dk), lambda b, h, t: (h, 0, 0)),
            pl.BlockSpec((1, dv), lambda b, h, t: (0, 0)),
        ],
        out_specs=pl.BlockSpec((T, dv), lambda b, h, t: (b * nt + t, h)),
        out_shape=jax.ShapeDtypeStruct((R, H * dv), BF16),
        scratch_shapes=[pltpu.VMEM((dk, dv), F32)],
        compiler_params=_params(("parallel", "parallel", "arbitrary")),
        name="gla",
    )(proj, proj, proj, proj, gr, w_gate2, gate_bias, o_gain.reshape(1, dv))


def kernel(x, meta_tokens, rel_bias, mixer_norm, mlp_norm, a_w_in, a_q_gain, a_k_gain, a_sinks, a_w_out,
           b_w_in, b_w_gate2, b_gate_bias, b_o_gain, b_w_out, mlp_w1, mlp_w2):
    bsz, seq, D = x.shape
    depth = mixer_norm.shape[0]
    L = LEAD + seq
    nb = L // BLOCK
    pad = jnp.zeros((bsz, LEAD - N_META, D), x.dtype)
    meta = jnp.broadcast_to(meta_tokens.astype(x.dtype)[None], (bsz, N_META, D))
    h = jnp.concatenate([pad, meta, x], axis=1).reshape(bsz * L, D)

    bias_band, bias_meta = _attention_bias_tables(rel_bias, nb)
    dk = b_w_gate2.shape[-1] // B_HEADS
    b_main = 2 * B_HEADS * dk + 2 * D

    for layer in range(depth):
        idx = layer // N_MIXERS
        if layer % N_MIXERS == 0:
            proj = norm_matmul(h, mixer_norm[layer], a_w_in[idx].astype(BF16), F32)
            o = swa_attention(proj, bsz, L, a_q_gain[idx], a_k_gain[idx], a_sinks[idx], bias_band, bias_meta)
            h = matmul_residual(o, a_w_out[idx].astype(BF16), h)
        else:
            w_in = b_w_in[idx]
            proj = norm_matmul(h, mixer_norm[layer], w_in[:, :b_main].astype(BF16), F32)
            w_gr = jnp.pad(w_in[:, b_main:], ((0, 0), (0, LANES - B_GATE_RANK))).astype(BF16)
            gr = norm_matmul(h, mixer_norm[layer], w_gr, F32)
            w2 = jnp.pad(b_w_gate2[idx], ((0, LANES - B_GATE_RANK), (0, 0))).astype(BF16)
            w2 = jnp.transpose(w2.reshape(LANES, B_HEADS, dk), (1, 0, 2))
            gb = b_gate_bias[idx].reshape(B_HEADS, 1, dk)
            o = gla(proj, gr, w2, gb, b_o_gain[idx], bsz, L)
            h = matmul_residual(o, b_w_out[idx].astype(BF16), h)
        h = mlp(h, mlp_norm[layer], mlp_w1[layer].astype(BF16), mlp_w2[layer].astype(BF16))
    return h.reshape(bsz, L, D)[:, LEAD:]
```
